```python
import math
import jax, jax.numpy as jnp
from jax import lax
import numpy as np

D_MODEL = 1024
BATCH = 4
SEQ = 8192
DEPTH = 4

CHUNK = 64
N_MIXERS = 3
EPS = 1e-6

M_EXPAND = 2
M_D_INNER = M_EXPAND * D_MODEL
M_HEAD_DIM = 64
M_HEADS = M_D_INNER // M_HEAD_DIM
M_GROUPS = 8
M_HPG = M_HEADS // M_GROUPS
M_STATE = 128
M_CONV = 4
M_BC_DIM = M_GROUPS * M_STATE
M_CONV_DIM = M_D_INNER + 2 * M_BC_DIM
M_IN_DIM = M_D_INNER + M_CONV_DIM + M_HEADS

H_EXPAND = 128
H_HEADS = D_MODEL // H_EXPAND
H_KDIM = H_EXPAND
H_VDIM = D_MODEL // H_HEADS
H_IN_DIM = 4 * D_MODEL

S_HEAD_DIM = 64
S_HEADS = D_MODEL // S_HEAD_DIM
S_QBLOCK = 128
S_IN_DIM = 3 * D_MODEL

FFN_HIDDEN = 256 * math.ceil(8 * D_MODEL / 3 / 256)
N_MOD = 6

kernel_name = "hybrid_ssd_hgrn2_stickbreak_trunk"


def rms_norm(x, g):
    xf = x.astype(jnp.float32)
    y = xf * lax.rsqrt(jnp.mean(xf * xf, axis=-1, keepdims=True) + EPS)
    return (y * g.astype(jnp.float32)).astype(x.dtype)


def modulate(x, g, shift, scale):
    return rms_norm(x, g) * (1 + scale[:, None, :]) + shift[:, None, :]


def to_chunks(a):
    b, s = a.shape[:2]
    a = a.reshape(b, s // CHUNK, CHUNK, *a.shape[2:])
    return jnp.moveaxis(a, 1, 0)


def from_chunks(a):
    a = jnp.moveaxis(a, 0, 1)
    return a.reshape(a.shape[0], -1, *a.shape[3:])


def causal_depthwise_conv(u, w, b):
    k_width, ch = w.shape
    y = lax.conv_general_dilated(
        u, w[:, None, :].astype(u.dtype), window_strides=(1,),
        padding=((k_width - 1, 0),), dimension_numbers=("NWC", "WIO", "NWC"),
        feature_group_count=ch)
    return y + b


def ssd_chunked_scan(xs, dt, a, bm, cm):
    bsz = xs.shape[0]
    causal = jnp.tril(jnp.ones((CHUNK, CHUNK), dtype=bool))

    def step(state, inp):
        x_c, dt_c, b_c, c_c = inp
        cum = jnp.cumsum(dt_c * a, axis=1)
        seg = cum[:, :, None] - cum[:, None, :]
        decay = jnp.exp(jnp.where(causal[None, :, :, None, None], seg, -jnp.inf))
        cb = jnp.einsum('btgn,bsgn->btsg', c_c, b_c)
        xdt = x_c * dt_c[..., None]
        y_intra = jnp.einsum('btsg,btsgh,bsghp->btghp', cb, decay, xdt)
        y_inter = jnp.einsum('btgn,bghpn->btghp', c_c, state) * jnp.exp(cum)[..., None]
        to_end = jnp.exp(cum[:, -1:] - cum)
        new_state = (state * jnp.exp(cum[:, -1])[..., None, None]
                     + jnp.einsum('bsgn,bsgh,bsghp->bghpn', b_c, to_end, xdt))
        return new_state, y_intra + y_inter

    state0 = jnp.zeros((bsz, M_GROUPS, M_HPG, M_HEAD_DIM, M_STATE), jnp.float32)
    _, ys = lax.scan(step, state0, (to_chunks(xs), to_chunks(dt), to_chunks(bm), to_chunks(cm)))
    return from_chunks(ys)


def mamba2_mixer(h, w_in, conv_w, conv_b, dt_bias, a_log, d_skip, norm_g, w_out):
    bsz, seq, _ = h.shape
    z, xbc, dt = jnp.split(h @ w_in, [M_D_INNER, M_D_INNER + M_CONV_DIM], axis=-1)
    xbc = jax.nn.silu(causal_depthwise_conv(xbc, conv_w, conv_b))
    xs, bm, cm = jnp.split(xbc, [M_D_INNER, M_D_INNER + M_BC_DIM], axis=-1)
    xs = xs.reshape(bsz, seq, M_GROUPS, M_HPG, M_HEAD_DIM)
    bm = bm.reshape(bsz, seq, M_GROUPS, M_STATE)
    cm = cm.reshape(bsz, seq, M_GROUPS, M_STATE)
    dt = jax.nn.softplus((dt + dt_bias).astype(jnp.float32)).reshape(bsz, seq, M_GROUPS, M_HPG)
    a = -jnp.exp(a_log.astype(jnp.float32)).reshape(M_GROUPS, M_HPG)
    y = ssd_chunked_scan(xs, dt, a, bm, cm)
    y = y + d_skip.reshape(M_GROUPS, M_HPG)[:, :, None] * xs
    y = y.reshape(bsz, seq, M_D_INNER)
    y = rms_norm(y * jax.nn.silu(z), norm_g)
    return (y @ w_out).astype(h.dtype)


def gla_chunked_scan(q, k, v, log_f):
    bsz = q.shape[0]
    causal = jnp.tril(jnp.ones((CHUNK, CHUNK), dtype=bool))

    def step(state, inp):
        q_c, k_c, v_c, g_c = inp
        cum = jnp.cumsum(g_c, axis=1)
        seg = cum[:, :, None] - cum[:, None, :]
        decay = jnp.exp(jnp.where(causal[None, :, :, None, None], seg, -jnp.inf))
        scores = jnp.einsum('bthk,btshk,bshk->btsh', q_c, decay, k_c)
        o_intra = jnp.einsum('btsh,bshv->bthv', scores, v_c)
        o_inter = jnp.einsum('bthk,bhkv->bthv', q_c * jnp.exp(cum), state)
        k_end = k_c * jnp.exp(cum[:, -1:] - cum)
        new_state = (state * jnp.exp(cum[:, -1])[..., None]
                     + jnp.einsum('bshk,bshv->bhkv', k_end, v_c))
        return new_state, o_intra + o_inter

    state0 = jnp.zeros((bsz, H_HEADS, H_KDIM, H_VDIM), jnp.float32)
    _, os_ = lax.scan(step, state0, (to_chunks(q), to_chunks(k), to_chunks(v), to_chunks(log_f)))
    return from_chunks(os_)


def hgrn2_mixer(h, w_in, lower_bound, norm_g, w_out):
    bsz, seq, _ = h.shape
    q, f, i, g = jnp.split(h @ w_in, 4, axis=-1)
    q = jax.nn.silu(q).reshape(bsz, seq, H_HEADS, H_KDIM)
    f = f.astype(jnp.float32)
    lb = lower_bound.astype(jnp.float32)
    log_f = jnp.logaddexp(jnp.log(lb), jnp.log1p(-lb) + jax.nn.log_sigmoid(f))
    k = (1 - lb) * jax.nn.sigmoid(-f)
    log_f = log_f.reshape(bsz, seq, H_HEADS, H_KDIM)
    k = k.reshape(bsz, seq, H_HEADS, H_KDIM)
    v = i.reshape(bsz, seq, H_HEADS, H_VDIM)
    o = gla_chunked_scan(q, k, v, log_f)
    o = rms_norm(o, jnp.ones((H_VDIM,), jnp.float32)).reshape(bsz, seq, D_MODEL) * norm_g
    o = o * jax.nn.silu(g)
    return (o @ w_out).astype(h.dtype)


def stick_breaking_mixer(h, w_in, w_out):
    bsz, seq, _ = h.shape
    q, k, v = jnp.split(h @ w_in, 3, axis=-1)
    split_heads = lambda t: t.reshape(bsz, seq, S_HEADS, S_HEAD_DIM).transpose(0, 2, 1, 3)
    q, k, v = split_heads(q), split_heads(k), split_heads(v)
    scale = S_HEAD_DIM ** -0.5
    outs = []
    for qb in range(seq // S_QBLOCK):
        start, end = qb * S_QBLOCK, (qb + 1) * S_QBLOCK
        z = jnp.einsum('bhqd,bhkd->bhqk', q[:, :, start:end], k[:, :, :end]).astype(jnp.float32) * scale
        q_pos = start + jnp.arange(S_QBLOCK)
        k_pos = jnp.arange(end)
        before = k_pos[None, :] < q_pos[:, None]
        log_1mb = jnp.where(before, jax.nn.log_sigmoid(-z), 0.0)
        suffix = lax.cumsum(log_1mb, axis=3, reverse=True) - log_1mb
        weights = jnp.where(before, jnp.exp(jax.nn.log_sigmoid(z) + suffix), 0.0)
        outs.append(jnp.einsum('bhqk,bhkd->bhqd', weights.astype(v.dtype), v[:, :, :end]))
    o = jnp.concatenate(outs, axis=2).transpose(0, 2, 1, 3).reshape(bsz, seq, D_MODEL)
    return (o @ w_out).astype(h.dtype)


def swiglu(h, w_in, w_out):
    gate, up = jnp.split(h @ w_in, 2, axis=-1)
    return (jax.nn.silu(gate) * up) @ w_out


def setup_inputs(seed: int = 0) -> dict:
    key = jax.random.key(seed)
    ks = jax.random.split(key, 24)
    f32 = jnp.float32
    n_a = len(range(0, DEPTH, N_MIXERS))
    n_b = len(range(1, DEPTH, N_MIXERS))
    n_c = len(range(2, DEPTH, N_MIXERS))

    def dense(k, shape, fan_in, mult=1.0):
        return jax.random.normal(k, shape, f32) * (mult * fan_in ** -0.5)

    def gain(k, shape):
        return 1.0 + 0.02 * jax.random.normal(k, shape, f32)

    dt_init = jnp.exp(jax.random.uniform(ks[9], (n_a, M_HEADS), f32, math.log(1e-3), math.log(1e-1)))
    return {
        "x": jax.random.normal(ks[0], (BATCH, SEQ, D_MODEL), f32),
        "c": jax.random.normal(ks[1], (BATCH, D_MODEL), f32),
        "ada_w": dense(ks[2], (DEPTH, D_MODEL, N_MOD * D_MODEL), D_MODEL, 0.5),
        "ada_b": 0.02 * jax.random.normal(ks[3], (DEPTH, N_MOD * D_MODEL), f32),
        "norm_g": gain(ks[4], (DEPTH, 4, D_MODEL)),
        "ffn_w_in": dense(ks[5], (DEPTH, D_MODEL, 2 * FFN_HIDDEN), D_MODEL),
        "ffn_w_out": dense(ks[6], (DEPTH, FFN_HIDDEN, D_MODEL), FFN_HIDDEN),
        "m_w_in": dense(ks[7], (n_a, D_MODEL, M_IN_DIM), D_MODEL),
        "m_conv_w": dense(ks[8], (n_a, M_CONV, M_CONV_DIM), M_CONV),
        "m_conv_b": 0.02 * jax.random.normal(ks[10], (n_a, M_CONV_DIM), f32),
        "m_dt_bias": dt_init + jnp.log(-jnp.expm1(-dt_init)),
        "m_a_log": jnp.log(jax.random.uniform(ks[11], (n_a, M_HEADS), f32, 1.0, 16.0)),
        "m_d": 1.0 + 0.1 * jax.random.normal(ks[12], (n_a, M_HEADS), f32),
        "m_norm_g": gain(ks[13], (n_a, M_D_INNER)),
        "m_w_out": dense(ks[14], (n_a, M_D_INNER, D_MODEL), M_D_INNER),
        "h_w_in": dense(ks[15], (n_b, D_MODEL, H_IN_DIM), D_MODEL),
        "h_lb_logits": 0.1 * jax.random.normal(ks[16], (DEPTH, D_MODEL), f32),
        "h_norm_g": gain(ks[17], (n_b, D_MODEL)),
        "h_w_out": dense(ks[18], (n_b, D_MODEL, D_MODEL), D_MODEL),
        "s_w_in": dense(ks[19], (n_c, D_MODEL, S_IN_DIM), D_MODEL),
        "s_w_out": dense(ks[20], (n_c, D_MODEL, D_MODEL), D_MODEL),
    }


def reference(x, c, ada_w, ada_b, norm_g, ffn_w_in, ffn_w_out,
              m_w_in, m_conv_w, m_conv_b, m_dt_bias, m_a_log, m_d, m_norm_g, m_w_out,
              h_w_in, h_lb_logits, h_norm_g, h_w_out, s_w_in, s_w_out):
    p = jax.nn.softmax(h_lb_logits.astype(jnp.float32), axis=0)
    lower_bounds = jnp.cumsum(p, axis=0) - p[0]
    cond = jax.nn.silu(c)
    for layer in range(DEPTH):
        mod = cond @ ada_w[layer] + ada_b[layer]
        sh_m, sc_m, gt_m, sh_f, sc_f, gt_f = jnp.split(mod, N_MOD, axis=-1)
        hm = modulate(x, norm_g[layer, 0], sh_m, sc_m)
        kind, j = layer % N_MIXERS, layer // N_MIXERS
        if kind == 0:
            y = mamba2_mixer(hm, m_w_in[j], m_conv_w[j], m_conv_b[j], m_dt_bias[j],
                             m_a_log[j], m_d[j], m_norm_g[j], m_w_out[j])
        elif kind == 1:
            y = hgrn2_mixer(hm, h_w_in[j], lower_bounds[layer], h_norm_g[j], h_w_out[j])
        else:
            y = stick_breaking_mixer(hm, s_w_in[j], s_w_out[j])
        x = x + (1 + gt_m[:, None, :]) * rms_norm(y, norm_g[layer, 1])
        hf = modulate(x, norm_g[layer, 2], sh_f, sc_f)
        y = swiglu(hf, ffn_w_in[layer], ffn_w_out[layer])
        x = x + (1 + gt_f[:, None, :]) * rms_norm(y, norm_g[layer, 3])
    return x
```

```python
import functools
import math

import jax
import jax.numpy as jnp
import numpy as np
from jax import lax
from jax.experimental import pallas as pl
from jax.experimental.pallas import tpu as pltpu

EPS = 1e-6
N_MIXERS = 3
N_MOD = 6

M_HEAD_DIM = 64
M_GROUPS = 8
M_STATE = 128
M_CONV = 4
H_KDIM = 128
S_HEAD_DIM = 64

LANES = 128
SUBLANES = 8
VMEM_LIMIT = 56 * 1024 * 1024

UNDERFLOW_SUM = 105.0

BF16 = jnp.bfloat16
F32 = jnp.float32


def _cparams(sem):
    return pltpu.CompilerParams(dimension_semantics=sem, vmem_limit_bytes=VMEM_LIMIT)


def _silu(x):
    return x * (1.0 / (1.0 + jnp.exp(-x)))


def _sigmoid(x):
    return 1.0 / (1.0 + jnp.exp(-x))


def _softplus(x):
    return jnp.maximum(x, 0.0) + jnp.log(1.0 + jnp.exp(-jnp.abs(x)))


def _rms(x):
    return x * lax.rsqrt(jnp.mean(x * x, axis=-1, keepdims=True) + EPS)


def _split3(x):
    a = x.astype(BF16)
    r = x - a.astype(F32)
    b = r.astype(BF16)
    c = (r - b.astype(F32)).astype(BF16)
    return a, b, c


def _dot(a, b):
    return jnp.dot(a, b, preferred_element_type=F32)


def _dot_nt(a, b):
    return lax.dot_general(a, b, (((1,), (1,)), ((), ())), preferred_element_type=F32)


def _dot_tn(a, b):
    return lax.dot_general(a, b, (((0,), (0,)), ((), ())), preferred_element_type=F32)


def _dot_sel(m_bf16, x_f32):
    a, b, c = _split3(x_f32)
    return _dot(m_bf16, a) + _dot(m_bf16, b) + _dot(m_bf16, c)


def _adaln_kernel(c_ref, w_ref, b_ref, o_ref):
    cond = _silu(c_ref[...]).astype(BF16)
    o_ref[...] = _dot(cond, w_ref[...].astype(BF16)) + b_ref[...]


def _adaln(c_pad, ada_w, ada_b):
    depth, d, n = ada_w.shape
    rows = c_pad.shape[0]
    tn = 1536
    return pl.pallas_call(
        _adaln_kernel,
        grid=(depth, n // tn),
        in_specs=[
            pl.BlockSpec((rows, d), lambda l, j: (0, 0)),
            pl.BlockSpec((None, d, tn), lambda l, j: (l, 0, j)),
            pl.BlockSpec((None, 1, tn), lambda l, j: (l, 0, j)),
        ],
        out_specs=pl.BlockSpec((None, rows, tn), lambda l, j: (l, 0, j)),
        out_shape=jax.ShapeDtypeStruct((depth, rows, n), F32),
        compiler_params=_cparams(("parallel", "parallel")),
        name="adaln",
    )(c_pad, ada_w, ada_b.reshape(depth, 1, n))


def _inproj_kernel(x_ref, g_ref, sh_ref, sc_ref, w_ref, o_ref, h_ref):
    @pl.when(pl.program_id(2) == 0)
    def _():
        x = x_ref[...]
        h = _rms(x) * g_ref[...]
        h = h * (1.0 + sc_ref[...]) + sh_ref[...]
        h_ref[...] = h.astype(BF16)

    o_ref[...] = _dot(h_ref[...], w_ref[...]).astype(o_ref.dtype)


def _inproj(x, g, shift, scale, w, tn, out_dtype, tm=512):
    bsz, seq, d = x.shape
    n = w.shape[1]
    return pl.pallas_call(
        _inproj_kernel,
        grid=(bsz, seq // tm, n // tn),
        in_specs=[
            pl.BlockSpec((None, tm, d), lambda b, i, j: (b, i, 0)),
            pl.BlockSpec((1, d), lambda b, i, j: (0, 0)),
            pl.BlockSpec((None, 1, d), lambda b, i, j: (b, 0, 0)),
            pl.BlockSpec((None, 1, d), lambda b, i, j: (b, 0, 0)),
            pl.BlockSpec((d, tn), lambda b, i, j: (0, j)),
        ],
        out_specs=pl.BlockSpec((None, tm, tn), lambda b, i, j: (b, i, j)),
        out_shape=jax.ShapeDtypeStruct((bsz, seq, n), out_dtype),
        scratch_shapes=[pltpu.VMEM((tm, d), BF16)],
        compiler_params=_cparams(("parallel", "parallel", "arbitrary")),
        name="inproj",
    )(x, g, shift, scale, w)


def _outproj_kernel(a_ref, w_ref, x_ref, g_ref, gt_ref, o_ref):
    y = _dot(a_ref[...], w_ref[...])
    o_ref[...] = x_ref[...] + (1.0 + gt_ref[...]) * (_rms(y) * g_ref[...])


def _outproj(a, w, x, g, gate, tm=512):
    bsz, seq, d = x.shape
    k = a.shape[-1]
    return pl.pallas_call(
        _outproj_kernel,
        grid=(bsz, seq // tm),
        in_specs=[
            pl.BlockSpec((None, tm, k), lambda b, i: (b, i, 0)),
            pl.BlockSpec((k, d), lambda b, i: (0, 0)),
            pl.BlockSpec((None, tm, d), lambda b, i: (b, i, 0)),
            pl.BlockSpec((1, d), lambda b, i: (0, 0)),
            pl.BlockSpec((None, 1, d), lambda b, i: (b, 0, 0)),
        ],
        out_specs=pl.BlockSpec((None, tm, d), lambda b, i: (b, i, 0)),
        out_shape=jax.ShapeDtypeStruct((bsz, seq, d), F32),
        compiler_params=_cparams(("parallel", "parallel")),
        name="outproj",
    )(a, w, x, g, gate)


def _ffn_kernel(x_ref, g2_ref, sh_ref, sc_ref, wg_ref, wu_ref, wo_ref, g3_ref, gt_ref, o_ref,
                *, chunk):
    x = x_ref[...]
    h = _rms(x) * g2_ref[...]
    h = (h * (1.0 + sc_ref[...]) + sh_ref[...]).astype(BF16)
    hidden = wg_ref.shape[1]
    y = jnp.zeros(x.shape, F32)
    for c0 in range(0, hidden, chunk):
        gate = _dot(h, wg_ref[:, c0:c0 + chunk])
        up = _dot(h, wu_ref[:, c0:c0 + chunk])
        act = (_silu(gate) * up).astype(BF16)
        y = y + _dot(act, wo_ref[c0:c0 + chunk, :])
    o_ref[...] = x + (1.0 + gt_ref[...]) * (_rms(y) * g3_ref[...])


def _ffn(x, g2, shift, scale, wg, wu, wo, g3, gate, tm=512):
    bsz, seq, d = x.shape
    hidden = wg.shape[1]
    chunk = hidden // 2 if (hidden // 2) % LANES == 0 else hidden
    const = lambda b, i: (0, 0)
    single = pl.Buffered(1)
    return pl.pallas_call(
        functools.partial(_ffn_kernel, chunk=chunk),
        grid=(bsz, seq // tm),
        in_specs=[
            pl.BlockSpec((None, tm, d), lambda b, i: (b, i, 0)),
            pl.BlockSpec((1, d), const),
            pl.BlockSpec((None, 1, d), lambda b, i: (b, 0, 0)),
            pl.BlockSpec((None, 1, d), lambda b, i: (b, 0, 0)),
            pl.BlockSpec((d, hidden), const, pipeline_mode=single),
            pl.BlockSpec((d, hidden), const, pipeline_mode=single),
            pl.BlockSpec((hidden, d), const, pipeline_mode=single),
            pl.BlockSpec((1, d), const),
            pl.BlockSpec((None, 1, d), lambda b, i: (b, 0, 0)),
        ],
        out_specs=pl.BlockSpec((None, tm, d), lambda b, i: (b, i, 0)),
        out_shape=jax.ShapeDtypeStruct((bsz, seq, d), F32),
        compiler_params=_cparams(("parallel", "parallel")),
        name="ffn",
    )(x, g2, shift, scale, wg, wu, wo, g3, gate)


def _stickbreak_kernel(q_ref, k_ref, v_ref, u_ref, o_ref, acc_ref, carry_ref, *, tq):
    qi = pl.program_id(2)
    lane = lax.broadcasted_iota(jnp.int32, (1, LANES), 1)
    first = lane < S_HEAD_DIM
    q = q_ref[...] * (S_HEAD_DIM ** -0.5)
    zero = jnp.zeros_like(q)
    q2 = jnp.concatenate([jnp.where(first, q, zero), jnp.where(first, zero, q)], axis=0)
    acc_ref[...] = jnp.zeros_like(acc_ref)
    carry_ref[...] = jnp.zeros_like(carry_ref)
    row = lax.broadcasted_iota(jnp.int32, (tq, tq), 0)
    col = lax.broadcasted_iota(jnp.int32, (tq, tq), 1)
    before1 = col < row
    before = jnp.concatenate([before1, before1], axis=0)

    def block(j, masked):
        start = pl.multiple_of(j * tq, tq)
        kj = k_ref[pl.ds(start, tq), :]
        vj = v_ref[pl.ds(start, tq), :]
        z = _dot_nt(q2, kj)
        lg = jnp.log(1.0 + jnp.exp(-jnp.abs(z)))
        sp = jnp.maximum(z, 0.0) + lg
        if masked:
            sp = jnp.where(before, sp, 0.0)
        hi = sp.astype(BF16)
        lo = (sp - hi.astype(F32)).astype(BF16)
        suffix = _dot(jnp.concatenate([hi, lo], axis=1), u_ref[...])
        carry = carry_ref[...]
        arg = (jnp.minimum(z, 0.0) - lg) - (suffix + carry)
        w = jnp.exp(arg)
        if masked:
            w = jnp.where(before, w, 0.0)
        acc_ref[...] += _dot(w.astype(BF16), vj)
        new_carry = carry + suffix[:, 0:1] + sp[:, 0:1]
        carry_ref[...] = new_carry
        return jnp.min(new_carry)

    cmin = block(qi, True)

    def cond(state):
        j, c = state
        return jnp.logical_and(j >= 0, c <= UNDERFLOW_SUM)

    def body(state):
        j, _ = state
        return j - 1, block(j, False)

    lax.while_loop(cond, body, (qi - 1, cmin))
    acc = acc_ref[...]
    o_ref[...] = jnp.where(first, acc[:tq], acc[tq:]).astype(o_ref.dtype)


def _stickbreak(qkv, d_model, tq=128):
    bsz, seq, _ = qkv.shape
    pairs = d_model // LANES
    j = np.arange(2 * tq)[:, None] % tq
    s = np.arange(tq)[None, :]
    u = jnp.asarray((j > s).astype(np.float32), dtype=BF16)
    return pl.pallas_call(
        functools.partial(_stickbreak_kernel, tq=tq),
        grid=(bsz, pairs, seq // tq),
        in_specs=[
            pl.BlockSpec((None, tq, LANES), lambda b, p, i: (b, i, p)),
            pl.BlockSpec((None, seq, LANES), lambda b, p, i: (b, 0, pairs + p)),
            pl.BlockSpec((None, seq, LANES), lambda b, p, i: (b, 0, 2 * pairs + p)),
            pl.BlockSpec((2 * tq, tq), lambda b, p, i: (0, 0)),
        ],
        out_specs=pl.BlockSpec((None, tq, LANES), lambda b, p, i: (b, i, p)),
        out_shape=jax.ShapeDtypeStruct((bsz, seq, d_model), BF16),
        scratch_shapes=[pltpu.VMEM((2 * tq, LANES), F32), pltpu.VMEM((2 * tq, 1), F32)],
        compiler_params=_cparams(("parallel", "parallel", "arbitrary")),
        name="stickbreak",
    )(qkv, qkv, qkv, u)


def _mamba_kernel(xbc_ref, z_ref, dt_ref, cw_ref, cb_ref, dtb_ref, alog_ref, dsk_ref, ng_ref,
                  tri_ref, o_ref, cbuf_ref, state_ref, y_ref, *, chunk, d_inner):
    L = chunk
    n_pairs = d_inner // LANES
    bc = M_GROUPS * M_STATE

    @pl.when(pl.program_id(1) == 0)
    def _():
        cbuf_ref[0:SUBLANES, :] = jnp.zeros((SUBLANES, cbuf_ref.shape[1]), F32)
        state_ref[...] = jnp.zeros_like(state_ref)

    cbuf_ref[SUBLANES:SUBLANES + L, :] = xbc_ref[...].astype(F32)
    conv = cb_ref[...] + cw_ref[M_CONV - 1:M_CONV, :] * cbuf_ref[SUBLANES:SUBLANES + L, :]
    for k in range(M_CONV - 1):
        off = SUBLANES - (M_CONV - 1) + k
        conv = conv + cw_ref[k:k + 1, :] * cbuf_ref[off:off + L, :]
    cbuf_ref[0:SUBLANES, :] = cbuf_ref[L:L + SUBLANES, :]
    u = _silu(conv)

    dt = _softplus(dt_ref[...] + dtb_ref[...])
    a = -jnp.exp(alog_ref[...])
    cum = _dot_sel(tri_ref[...], dt * a)
    cum_t = cum.T
    lane = lax.broadcasted_iota(jnp.int32, (1, LANES), 1)
    first = lane < M_HEAD_DIM
    row = lax.broadcasted_iota(jnp.int32, (L, L), 0)
    col = lax.broadcasted_iota(jnp.int32, (L, L), 1)
    causal = col <= row

    for g in range(M_GROUPS):
        b_g = u[:, d_inner + g * M_STATE:d_inner + (g + 1) * M_STATE]
        c_g = u[:, d_inner + bc + g * M_STATE:d_inner + bc + (g + 1) * M_STATE]
        b_bf = b_g.astype(BF16)
        c_bf = c_g.astype(BF16)
        cb = _dot_nt(c_bf, b_bf)
        pairs_per_group = n_pairs // M_GROUPS
        for pp in range(pairs_per_group):
            p = g * pairs_per_group + pp
            h0 = 2 * p
            xp = u[:, p * LANES:(p + 1) * LANES]
            dtp = jnp.where(first, dt[:, h0:h0 + 1], dt[:, h0 + 1:h0 + 2])
            cump = jnp.where(first, cum[:, h0:h0 + 1], cum[:, h0 + 1:h0 + 2])
            xdt = xp * dtp
            ms = []
            for h in (h0, h0 + 1):
                seg = cum[:, h:h + 1] - cum_t[h:h + 1, :]
                dec = jnp.where(causal, jnp.exp(jnp.minimum(seg, 0.0)), 0.0)
                ms.append((cb * dec).astype(BF16))
            lhs = jnp.concatenate(ms, axis=1)
            xdt_bf = xdt.astype(BF16)
            zero = jnp.zeros_like(xdt_bf)
            rhs = jnp.concatenate([jnp.where(first, xdt_bf, zero),
                                   jnp.where(first, zero, xdt_bf)], axis=0)
            y_intra = _dot(lhs, rhs)
            st = state_ref[:, p * LANES:(p + 1) * LANES]
            ecum = jnp.exp(cump)
            y_inter = _dot(c_bf, st.astype(BF16)) * ecum
            y_ref[:, p * LANES:(p + 1) * LANES] = y_intra + y_inter + dsk_ref[:, p * LANES:(p + 1) * LANES] * xp
            last = cump[L - 1:L, :]
            to_end = jnp.exp(last - cump)
            upd = _dot_tn(b_bf, (xdt * to_end).astype(BF16))
            state_ref[:, p * LANES:(p + 1) * LANES] = st * jnp.exp(last) + upd

    y = y_ref[...] * _silu(z_ref[...].astype(F32))
    o_ref[...] = (_rms(y) * ng_ref[...]).astype(o_ref.dtype)


def _mamba_core(proj, conv_w, conv_b, dt_bias, a_log, d_skip, norm_g, d_inner, chunk=128):
    bsz, seq, _ = proj.shape
    conv_dim = conv_w.shape[1]
    tri = jnp.asarray(np.tril(np.ones((chunk, chunk), np.float32)), dtype=BF16)
    zb = conv_dim // d_inner
    dtb = (conv_dim + d_inner) // LANES
    const = lambda b, i: (0, 0)
    return pl.pallas_call(
        functools.partial(_mamba_kernel, chunk=chunk, d_inner=d_inner),
        grid=(bsz, seq // chunk),
        in_specs=[
            pl.BlockSpec((None, chunk, conv_dim), lambda b, i: (b, i, 0)),
            pl.BlockSpec((None, chunk, d_inner), lambda b, i: (b, i, zb)),
            pl.BlockSpec((None, chunk, LANES), lambda b, i: (b, i, dtb)),
            pl.BlockSpec((M_CONV, conv_dim), const),
            pl.BlockSpec((1, conv_dim), const),
            pl.BlockSpec((1, LANES), const),
            pl.BlockSpec((1, LANES), const),
            pl.BlockSpec((1, d_inner), const),
            pl.BlockSpec((1, d_inner), const),
            pl.BlockSpec((chunk, chunk), const),
        ],
        out_specs=pl.BlockSpec((None, chunk, d_inner), lambda b, i: (b, i, 0)),
        out_shape=jax.ShapeDtypeStruct((bsz, seq, d_inner), BF16),
        scratch_shapes=[
            pltpu.VMEM((chunk + 2 * SUBLANES, conv_dim), F32),
            pltpu.VMEM((M_STATE, d_inner), F32),
            pltpu.VMEM((chunk, d_inner), F32),
        ],
        compiler_params=_cparams(("parallel", "arbitrary")),
        name="mamba_core",
    )(proj, proj, proj, conv_w, conv_b, dt_bias, a_log, d_skip, norm_g, tri)


def _hgrn_level_tables(chunk):
    levels = []
    m = chunk // 2
    while m >= 1:
        levels.append(m)
        m //= 2
    t = np.arange(chunk)
    tri = (t[None, :] <= t[:, None]).astype(np.float32)
    mats = [tri]
    for m in levels:
        ref = (t // (2 * m)) * (2 * m) + m - 1
        sel = np.zeros((chunk, chunk), np.float32)
        sel[t, ref] = 1.0
        mats.append((np.eye(chunk, dtype=np.float32) - sel) @ tri)
    return levels, np.concatenate(mats, axis=0)


def _hgrn_kernel(p_ref, lbl_ref, ng_ref, seg_ref, o_ref, state_ref, *, chunk, d_model, layer,
                 levels):
    C = chunk
    heads = d_model // H_KDIM

    @pl.when(pl.program_id(1) == 0)
    def _():
        state_ref[...] = jnp.zeros_like(state_ref)

    logits = lbl_ref[...]
    mx = jnp.max(logits, axis=0, keepdims=True)
    ex = jnp.exp(logits - mx)
    den = jnp.sum(ex, axis=0, keepdims=True)
    lb = jnp.zeros((1, d_model), F32)
    for l in range(1, layer + 1):
        lb = lb + ex[l:l + 1, :] / den

    q = _silu(p_ref[:, 0:d_model].astype(F32))
    f = p_ref[:, d_model:2 * d_model].astype(F32)
    v = p_ref[:, 2 * d_model:3 * d_model].astype(F32)
    gate = p_ref[:, 3 * d_model:4 * d_model].astype(F32)

    a = jnp.log(lb)
    log_sig = jnp.minimum(f, 0.0) - jnp.log(1.0 + jnp.exp(-jnp.abs(f)))
    b = jnp.log(1.0 - lb) + log_sig
    log_f = jnp.maximum(a, b) + jnp.log(1.0 + jnp.exp(-jnp.abs(a - b)))
    k = (1.0 - lb) * _sigmoid(-f)

    segs = _dot_sel(seg_ref[...], log_f)
    G = segs[0:C]
    row = lax.broadcasted_iota(jnp.int32, (C, 1), 0)
    trow = lax.broadcasted_iota(jnp.int32, (C, C), 0)
    tcol = lax.broadcasted_iota(jnp.int32, (C, C), 1)

    scores = [jnp.where(trow == tcol, _dot_nt(
        (q[:, h * H_KDIM:(h + 1) * H_KDIM]).astype(BF16),
        (k[:, h * H_KDIM:(h + 1) * H_KDIM]).astype(BF16)), 0.0) for h in range(heads)]
    for li, m in enumerate(levels):
        d = segs[(li + 1) * C:(li + 2) * C]
        e = jnp.exp(-jnp.abs(d))
        shift = m.bit_length() - 1
        upper = ((row >> shift) & 1) == 1
        qe = jnp.where(upper, q * e, 0.0).astype(BF16)
        ke = jnp.where(upper, 0.0, k * e).astype(BF16)
        same = (trow >> (shift + 1)) == (tcol >> (shift + 1))
        for h in range(heads):
            sl = slice(h * H_KDIM, (h + 1) * H_KDIM)
            scores[h] = scores[h] + jnp.where(same, _dot_nt(qe[:, sl], ke[:, sl]), 0.0)

    g_last = G[C - 1:C, :]
    q_in = (q * jnp.exp(G)).astype(BF16)
    k_end = (k * jnp.exp(g_last - G)).astype(BF16)
    v_bf = v.astype(BF16)
    decay_last = jnp.exp(g_last)
    outs = []
    for h in range(heads):
        sl = slice(h * H_KDIM, (h + 1) * H_KDIM)
        st = state_ref[h]
        o_h = _dot(scores[h].astype(BF16), v_bf[:, sl]) + _dot_nt(q_in[:, sl], st.astype(BF16))
        state_ref[h] = st * decay_last[:, sl] + _dot_tn(v_bf[:, sl], k_end[:, sl])
        outs.append(_rms(o_h))
    o = jnp.concatenate(outs, axis=1) * ng_ref[...]
    o_ref[...] = (o * _silu(gate)).astype(o_ref.dtype)


def _hgrn_core(proj, lb_logits, norm_g, layer, d_model, chunk=64):
    bsz, seq, n = proj.shape
    depth = lb_logits.shape[0]
    heads = d_model // H_KDIM
    levels, seg = _hgrn_level_tables(chunk)
    seg = jnp.asarray(seg, dtype=BF16)
    const = lambda b, i: (0, 0)
    return pl.pallas_call(
        functools.partial(_hgrn_kernel, chunk=chunk, d_model=d_model, layer=layer,
                          levels=tuple(levels)),
        grid=(bsz, seq // chunk),
        in_specs=[
            pl.BlockSpec((None, chunk, n), lambda b, i: (b, i, 0)),
            pl.BlockSpec((depth, d_model), const),
            pl.BlockSpec((1, d_model), const),
            pl.BlockSpec(seg.shape, const),
        ],
        out_specs=pl.BlockSpec((None, chunk, d_model), lambda b, i: (b, i, 0)),
        out_shape=jax.ShapeDtypeStruct((bsz, seq, d_model), BF16),
        scratch_shapes=[pltpu.VMEM((heads, H_KDIM, H_KDIM), F32)],
        compiler_params=_cparams(("parallel", "arbitrary")),
        name="hgrn_core",
    )(proj, lb_logits, norm_g, seg)


def _pad_cols(a, n):
    return jnp.pad(a, ((0, 0), (0, n - a.shape[1])))


def kernel(x, c, ada_w, ada_b, norm_g, ffn_w_in, ffn_w_out, m_w_in, m_conv_w, m_conv_b, m_dt_bias,
           m_a_log, m_d, m_norm_g, m_w_out, h_w_in, h_lb_logits, h_norm_g, h_w_out, s_w_in,
           s_w_out):
    bsz, seq, d = x.shape
    depth = ada_w.shape[0]
    hidden = ffn_w_out.shape[1]
    d_inner = m_w_out.shape[1]
    m_heads = m_dt_bias.shape[1]
    conv_dim = m_conv_w.shape[2]

    rows = -(-bsz // SUBLANES) * SUBLANES
    c_pad = jnp.pad(c, ((0, rows - bsz), (0, 0)))
    mod = _adaln(c_pad, ada_w, ada_b)[:, :bsz]
    mod = mod.reshape(depth, bsz, N_MOD, 1, d)

    for layer in range(depth):
        sh_m, sc_m, gt_m, sh_f, sc_f, gt_f = (mod[layer, :, i] for i in range(N_MOD))
        g = norm_g[layer].reshape(4, 1, d)
        kind, j = layer % N_MIXERS, layer // N_MIXERS
        if kind == 0:
            w = m_w_in[j]
            w_z = w[:, :d_inner]
            w_xbc = w[:, d_inner:d_inner + conv_dim]
            w_dt = _pad_cols(w[:, d_inner + conv_dim:], LANES)
            w_cat = jnp.concatenate([w_xbc, w_z, w_dt], axis=1).astype(BF16)
            proj = _inproj(x, g[0], sh_m, sc_m, w_cat, tn=896, out_dtype=F32)
            a = _mamba_core(
                proj, m_conv_w[j], m_conv_b[j].reshape(1, conv_dim),
                _pad_cols(m_dt_bias[j].reshape(1, m_heads), LANES),
                _pad_cols(m_a_log[j].reshape(1, m_heads), LANES),
                jnp.repeat(m_d[j], M_HEAD_DIM).reshape(1, d_inner),
                m_norm_g[j].reshape(1, d_inner), d_inner)
            w_out = m_w_out[j]
        elif kind == 1:
            proj = _inproj(x, g[0], sh_m, sc_m, h_w_in[j].astype(BF16), tn=1024, out_dtype=F32)
            a = _hgrn_core(proj, h_lb_logits, h_norm_g[j].reshape(1, d), layer, d)
            w_out = h_w_out[j]
        else:
            qkv = _inproj(x, g[0], sh_m, sc_m, s_w_in[j].astype(BF16), tn=1024, out_dtype=BF16)
            a = _stickbreak(qkv, d)
            w_out = s_w_out[j]
        x = _outproj(a, w_out.astype(BF16), x, g[1], gt_m)
        w_in = ffn_w_in[layer]
        x = _ffn(x, g[2], sh_f, sc_f, w_in[:, :hidden].astype(BF16), w_in[:, hidden:].astype(BF16),
                 ffn_w_out[layer].astype(BF16), g[3], gt_f)
    return x
```

```python
import functools
import math

import jax
import jax.numpy as jnp
import numpy as np
from jax import lax
from jax.experimental import pallas as pl
from jax.experimental.pallas import tpu as pltpu

EPS = 1e-6
N_MIXERS = 3
N_MOD = 6

M_HEAD_DIM = 64
M_GROUPS = 8
M_STATE = 128
M_CONV = 4
CONV_HIST = 16
CONV_COLS = 2048
H_KDIM = 128
S_HEAD_DIM = 64

LANES = 128
SUBLANES = 8
VMEM_LIMIT = 56 * 1024 * 1024

UNDERFLOW_SUM = 105.0

LOG2E = math.log2(math.e)
NEG_BIG = -1e30

BF16 = jnp.bfloat16
F32 = jnp.float32


def _cparams(sem):
    return pltpu.CompilerParams(dimension_semantics=sem, vmem_limit_bytes=VMEM_LIMIT)


def _silu(x):
    return x * (1.0 / (1.0 + jnp.exp(-x)))


def _sigmoid(x):
    return 1.0 / (1.0 + jnp.exp(-x))


def _softplus(x):
    return jnp.maximum(x, 0.0) + jnp.log(1.0 + jnp.exp(-jnp.abs(x)))


def _rms(x):
    return x * lax.rsqrt(jnp.mean(x * x, axis=-1, keepdims=True) + EPS)


def _split3(x):
    a = x.astype(BF16)
    r = x - a.astype(F32)
    b = r.astype(BF16)
    c = (r - b.astype(F32)).astype(BF16)
    return a, b, c


def _dot(a, b):
    return jnp.dot(a, b, preferred_element_type=F32)


def _dot_nt(a, b):
    return lax.dot_general(a, b, (((1,), (1,)), ((), ())), preferred_element_type=F32)


def _dot_tn(a, b):
    return lax.dot_general(a, b, (((0,), (0,)), ((), ())), preferred_element_type=F32)


def _dot_sel(m_bf16, x_f32):
    a, b, c = _split3(x_f32)
    return _dot(m_bf16, a) + _dot(m_bf16, b) + _dot(m_bf16, c)


def _adaln_kernel(c_ref, w_ref, b_ref, o_ref):
    cond = _silu(c_ref[...]).astype(BF16)
    o_ref[...] = _dot(cond, w_ref[...].astype(BF16)) + b_ref[...]


def _adaln(c_pad, ada_w, ada_b):
    depth, d, n = ada_w.shape
    rows = c_pad.shape[0]
    tn = 1536
    return pl.pallas_call(
        _adaln_kernel,
        grid=(depth, n // tn),
        in_specs=[
            pl.BlockSpec((rows, d), lambda l, j: (0, 0)),
            pl.BlockSpec((None, d, tn), lambda l, j: (l, 0, j)),
            pl.BlockSpec((None, 1, tn), lambda l, j: (l, 0, j)),
        ],
        out_specs=pl.BlockSpec((None, rows, tn), lambda l, j: (l, 0, j)),
        out_shape=jax.ShapeDtypeStruct((depth, rows, n), F32),
        compiler_params=_cparams(("parallel", "parallel")),
        name="adaln",
    )(c_pad, ada_w, ada_b.reshape(depth, 1, n))


def _inproj_kernel(x_ref, g_ref, sh_ref, sc_ref, w_ref, *rest, tn, has_f32):
    if has_f32:
        wf_ref, o_ref, of_ref = rest
    else:
        (o_ref,) = rest
    x = x_ref[...]
    h = _rms(x) * g_ref[...]
    h = (h * (1.0 + sc_ref[...]) + sh_ref[...]).astype(BF16)
    for n0 in range(0, w_ref.shape[1], tn):
        o_ref[:, n0:n0 + tn] = _dot(h, w_ref[:, n0:n0 + tn]).astype(o_ref.dtype)
    if has_f32:
        of_ref[...] = _dot(h, wf_ref[...])


def _inproj(x, g, shift, scale, w, w_f32out=None, tn=1024, tm=512):
    bsz, seq, d = x.shape
    n = w.shape[1]
    has_f32 = w_f32out is not None
    const = lambda b, i: (0, 0)
    row = lambda b, i: (b, i, 0)
    per_batch = lambda b, i: (b, 0, 0)
    single = pl.Buffered(1)
    in_specs = [
        pl.BlockSpec((None, tm, d), row),
        pl.BlockSpec((1, d), const),
        pl.BlockSpec((None, 1, d), per_batch),
        pl.BlockSpec((None, 1, d), per_batch),
        pl.BlockSpec((d, n), const, pipeline_mode=single),
    ]
    out_specs = [pl.BlockSpec((None, tm, n), row)]
    out_shape = [jax.ShapeDtypeStruct((bsz, seq, n), BF16)]
    args = [x, g, shift, scale, w]
    if has_f32:
        nf = w_f32out.shape[1]
        in_specs.append(pl.BlockSpec((d, nf), const, pipeline_mode=single))
        out_specs.append(pl.BlockSpec((None, tm, nf), row))
        out_shape.append(jax.ShapeDtypeStruct((bsz, seq, nf), F32))
        args.append(w_f32out)
    outs = pl.pallas_call(
        functools.partial(_inproj_kernel, tn=tn, has_f32=has_f32),
        grid=(bsz, seq // tm),
        in_specs=in_specs,
        out_specs=out_specs,
        out_shape=out_shape,
        compiler_params=_cparams(("parallel", "parallel")),
        name="inproj",
    )(*args)
    return outs if has_f32 else outs[0]


def _outffn_kernel(a_ref, wm_ref, x_ref, g1_ref, gtm_ref, g2_ref, sh_ref, sc_ref, wg_ref, wu_ref,
                   wo_ref, g3_ref, gt_ref, o_ref, *, chunk):
    ym = _dot(a_ref[...], wm_ref[...])
    x = x_ref[...] + (1.0 + gtm_ref[...]) * (_rms(ym) * g1_ref[...])
    h = _rms(x) * g2_ref[...]
    h = (h * (1.0 + sc_ref[...]) + sh_ref[...]).astype(BF16)
    hidden = wg_ref.shape[1]
    y = jnp.zeros(x.shape, F32)
    for c0 in range(0, hidden, chunk):
        gate = _dot(h, wg_ref[:, c0:c0 + chunk])
        up = _dot(h, wu_ref[:, c0:c0 + chunk])
        act = (_silu(gate) * up).astype(BF16)
        y = y + _dot(act, wo_ref[c0:c0 + chunk, :])
    o_ref[...] = x + (1.0 + gt_ref[...]) * (_rms(y) * g3_ref[...])


def _outffn(a, wm, x, g1, gate_m, g2, shift, scale, wg, wu, wo, g3, gate_f, tm=512):
    bsz, seq, d = x.shape
    k = a.shape[-1]
    hidden = wg.shape[1]
    chunk = hidden // 2 if (hidden // 2) % LANES == 0 else hidden
    const = lambda b, i: (0, 0)
    row = lambda b, i: (b, i, 0)
    per_batch = lambda b, i: (b, 0, 0)
    single = pl.Buffered(1)
    return pl.pallas_call(
        functools.partial(_outffn_kernel, chunk=chunk),
        grid=(bsz, seq // tm),
        in_specs=[
            pl.BlockSpec((None, tm, k), row),
            pl.BlockSpec((k, d), const, pipeline_mode=single),
            pl.BlockSpec((None, tm, d), row),
            pl.BlockSpec((1, d), const),
            pl.BlockSpec((None, 1, d), per_batch),
            pl.BlockSpec((1, d), const),
            pl.BlockSpec((None, 1, d), per_batch),
            pl.BlockSpec((None, 1, d), per_batch),
            pl.BlockSpec((d, hidden), const, pipeline_mode=single),
            pl.BlockSpec((d, hidden), const, pipeline_mode=single),
            pl.BlockSpec((hidden, d), const, pipeline_mode=single),
            pl.BlockSpec((1, d), const),
            pl.BlockSpec((None, 1, d), per_batch),
        ],
        out_specs=pl.BlockSpec((None, tm, d), row),
        out_shape=jax.ShapeDtypeStruct((bsz, seq, d), F32),
        compiler_params=_cparams(("parallel", "parallel")),
        name="outffn",
    )(a, wm, x, g1, gate_m, g2, shift, scale, wg, wu, wo, g3, gate_f)


def _stickbreak_kernel(q_ref, k_ref, v_ref, u_ref, o_ref, q2_ref, acc_ref, carry_ref, *, tq, hp):
    qi = pl.program_id(2)
    lane = lax.broadcasted_iota(jnp.int32, (1, LANES), 1)
    first = lane < S_HEAD_DIM
    for p in range(hp):
        q = q_ref[:, p * LANES:(p + 1) * LANES] * (S_HEAD_DIM ** -0.5)
        zero = jnp.zeros_like(q)
        q2_ref[p] = jnp.concatenate([jnp.where(first, q, zero), jnp.where(first, zero, q)], axis=0)
    acc_ref[...] = jnp.zeros_like(acc_ref)
    carry_ref[...] = jnp.zeros_like(carry_ref)
    rows = 2 * tq * hp
    row = lax.broadcasted_iota(jnp.int32, (rows, tq), 0)
    col = lax.broadcasted_iota(jnp.int32, (rows, tq), 1)
    before = col < (row & (tq - 1))

    def block(j, masked):
        start = pl.multiple_of(j * tq, tq)
        z = jnp.concatenate(
            [_dot_nt(q2_ref[p], k_ref[pl.ds(start, tq), p * LANES:(p + 1) * LANES])
             for p in range(hp)], axis=0)
        lg = jnp.log(1.0 + jnp.exp(-jnp.abs(z)))
        sp = jnp.maximum(z, 0.0) + lg
        if masked:
            sp = jnp.where(before, sp, 0.0)
        hi = sp.astype(BF16)
        lo = (sp - hi.astype(F32)).astype(BF16)
        suffix = _dot(jnp.concatenate([hi, lo], axis=1), u_ref[...])
        carry = carry_ref[...]
        arg = z - ((sp + suffix) + carry)
        w = jnp.exp(arg)
        if masked:
            w = jnp.where(before, w, 0.0)
        w = w.astype(BF16)
        for p in range(hp):
            rs = slice(p * 2 * tq, (p + 1) * 2 * tq)
            acc_ref[rs, :] += _dot(w[rs], v_ref[pl.ds(start, tq), p * LANES:(p + 1) * LANES])
        new_carry = carry + suffix[:, 0:1] + sp[:, 0:1]
        carry_ref[...] = new_carry
        return jnp.min(new_carry)

    cmin = block(qi, True)

    def cond(state):
        j, c = state
        return jnp.logical_and(j >= 0, c <= UNDERFLOW_SUM)

    def body(state):
        j, _ = state
        return j - 1, block(j, False)

    lax.while_loop(cond, body, (qi - 1, cmin))
    for p in range(hp):
        r0 = p * 2 * tq
        o_ref[:, p * LANES:(p + 1) * LANES] = jnp.where(
            first, acc_ref[r0:r0 + tq, :], acc_ref[r0 + tq:r0 + 2 * tq, :]).astype(o_ref.dtype)


def _stickbreak(qkv, d_model, tq=128, hp=8):
    bsz, seq, _ = qkv.shape
    groups = d_model // (hp * LANES)
    j = np.arange(2 * tq)[:, None] % tq
    s = np.arange(tq)[None, :]
    u = jnp.asarray((j > s).astype(np.float32), dtype=BF16)
    single = pl.Buffered(1)
    return pl.pallas_call(
        functools.partial(_stickbreak_kernel, tq=tq, hp=hp),
        grid=(bsz, groups, seq // tq),
        in_specs=[
            pl.BlockSpec((None, tq, hp * LANES), lambda b, g, i: (b, i, g)),
            pl.BlockSpec((None, seq, hp * LANES), lambda b, g, i: (b, 0, groups + g),
                         pipeline_mode=single),
            pl.BlockSpec((None, seq, hp * LANES), lambda b, g, i: (b, 0, 2 * groups + g),
                         pipeline_mode=single),
            pl.BlockSpec((2 * tq, tq), lambda b, g, i: (0, 0)),
        ],
        out_specs=pl.BlockSpec((None, tq, hp * LANES), lambda b, g, i: (b, i, g)),
        out_shape=jax.ShapeDtypeStruct((bsz, seq, d_model), BF16),
        scratch_shapes=[pltpu.VMEM((hp, 2 * tq, LANES), BF16),
                        pltpu.VMEM((hp * 2 * tq, LANES), F32),
                        pltpu.VMEM((hp * 2 * tq, 1), F32)],
        compiler_params=_cparams(("parallel", "parallel", "arbitrary")),
        name="stickbreak",
    )(qkv, qkv, qkv, u)


def _mamba_kernel(xbc_ref, z_ref, dt_ref, cw_ref, cb_ref, dtb_ref, alog_ref, dsk_ref, ng_ref,
                  tri_ref, shift_ref, o_ref, cbuf_ref, state_ref, y_ref, ux_ref, ubc_ref,
                  *, chunk, d_inner):
    L = chunk
    n_pairs = d_inner // LANES
    bc = M_GROUPS * M_STATE
    conv_dim = d_inner + 2 * bc

    @pl.when(pl.program_id(1) == 0)
    def _():
        cbuf_ref[0:L, :] = jnp.zeros((L, cbuf_ref.shape[1]), BF16)
        state_ref[...] = jnp.zeros_like(state_ref)

    cbuf_ref[L:2 * L, :] = xbc_ref[...]
    for c0 in range(0, conv_dim, CONV_COLS):
        cs = slice(c0, c0 + CONV_COLS)
        shifted = _dot(shift_ref[...], cbuf_ref[:, cs])
        conv = cb_ref[:, cs] + cw_ref[M_CONV - 1:M_CONV, cs] * xbc_ref[:, cs].astype(F32)
        for k in range(M_CONV - 1):
            conv = conv + cw_ref[k:k + 1, cs] * shifted[k * L:(k + 1) * L]
        u = _silu(conv)
        if c0 < d_inner:
            ux_ref[:, cs] = u
        else:
            ubc_ref[:, c0 - d_inner:c0 - d_inner + CONV_COLS] = u.astype(BF16)
    cbuf_ref[L - CONV_HIST:L, :] = cbuf_ref[2 * L - CONV_HIST:2 * L, :]

    dt = _softplus(dt_ref[...] + dtb_ref[...])
    a = -jnp.exp(alog_ref[...])
    cum = _dot_sel(tri_ref[...], dt * (a * LOG2E))
    cum_t = cum.T
    lane = lax.broadcasted_iota(jnp.int32, (1, LANES), 1)
    first = lane < M_HEAD_DIM
    row = lax.broadcasted_iota(jnp.int32, (L, L), 0)
    col = lax.broadcasted_iota(jnp.int32, (L, L), 1)
    causal = col <= row

    for g in range(M_GROUPS):
        b_bf = ubc_ref[:, g * M_STATE:(g + 1) * M_STATE]
        c_bf = ubc_ref[:, bc + g * M_STATE:bc + (g + 1) * M_STATE]
        cb = _dot_nt(c_bf, b_bf)
        pairs_per_group = n_pairs // M_GROUPS
        for pp in range(pairs_per_group):
            p = g * pairs_per_group + pp
            h0 = 2 * p
            ps = slice(p * LANES, (p + 1) * LANES)
            xp = ux_ref[:, ps]
            dtp = jnp.where(first, dt[:, h0:h0 + 1], dt[:, h0 + 1:h0 + 2])
            cump = jnp.where(first, cum[:, h0:h0 + 1], cum[:, h0 + 1:h0 + 2])
            xdt = xp * dtp
            ms = []
            for h in (h0, h0 + 1):
                seg = cum[:, h:h + 1] - cum_t[h:h + 1, :]
                dec = jnp.exp2(jnp.where(causal, seg, NEG_BIG))
                ms.append((cb * dec).astype(BF16))
            lhs = jnp.concatenate(ms, axis=1)
            xdt_bf = xdt.astype(BF16)
            zero = jnp.zeros_like(xdt_bf)
            rhs = jnp.concatenate([jnp.where(first, xdt_bf, zero),
                                   jnp.where(first, zero, xdt_bf)], axis=0)
            y_intra = _dot(lhs, rhs)
            st = state_ref[:, ps]
            ecum = jnp.exp2(cump)
            y_inter = _dot(c_bf, st.astype(BF16)) * ecum
            y = y_intra + y_inter + dsk_ref[:, ps] * xp
            last = cump[L - 1:L, :]
            to_end = jnp.exp2(last - cump)
            upd = _dot_tn(b_bf, (xdt * to_end).astype(BF16))
            state_ref[:, ps] = st * jnp.exp2(last) + upd
            y_ref[:, ps] = y

    yz = y_ref[...] * _silu(z_ref[...].astype(F32))
    o_ref[...] = (_rms(yz) * ng_ref[...]).astype(o_ref.dtype)


def _mamba_core(proj, dt, conv_w, conv_b, dt_bias, a_log, d_skip, norm_g, d_inner, chunk=128):
    bsz, seq, _ = proj.shape
    conv_dim = conv_w.shape[1]
    tri = jnp.asarray(np.tril(np.ones((chunk, chunk), np.float32)), dtype=BF16)
    shift = np.zeros(((M_CONV - 1) * chunk, 2 * chunk), np.float32)
    t = np.arange(chunk)
    for k in range(M_CONV - 1):
        shift[k * chunk + t, chunk + t - (M_CONV - 1) + k] = 1.0
    shift = jnp.asarray(shift, dtype=BF16)
    zb = conv_dim // d_inner
    const = lambda b, i: (0, 0)
    return pl.pallas_call(
        functools.partial(_mamba_kernel, chunk=chunk, d_inner=d_inner),
        grid=(bsz, seq // chunk),
        in_specs=[
            pl.BlockSpec((None, chunk, conv_dim), lambda b, i: (b, i, 0)),
            pl.BlockSpec((None, chunk, d_inner), lambda b, i: (b, i, zb)),
            pl.BlockSpec((None, chunk, LANES), lambda b, i: (b, i, 0)),
            pl.BlockSpec((M_CONV, conv_dim), const),
            pl.BlockSpec((1, conv_dim), const),
            pl.BlockSpec((1, LANES), const),
            pl.BlockSpec((1, LANES), const),
            pl.BlockSpec((1, d_inner), const),
            pl.BlockSpec((1, d_inner), const),
            pl.BlockSpec((chunk, chunk), const),
            pl.BlockSpec(shift.shape, const),
        ],
        out_specs=pl.BlockSpec((None, chunk, d_inner), lambda b, i: (b, i, 0)),
        out_shape=jax.ShapeDtypeStruct((bsz, seq, d_inner), BF16),
        scratch_shapes=[
            pltpu.VMEM((2 * chunk, conv_dim), BF16),
            pltpu.VMEM((M_STATE, d_inner), F32),
            pltpu.VMEM((chunk, d_inner), F32),
            pltpu.VMEM((chunk, d_inner), F32),
            pltpu.VMEM((chunk, conv_dim - d_inner), BF16),
        ],
        compiler_params=_cparams(("parallel", "arbitrary")),
        name="mamba_core",
    )(proj, proj, dt, conv_w, conv_b, dt_bias, a_log, d_skip, norm_g, tri, shift)


def _hgrn_level_tables(chunk):
    levels = []
    m = chunk // 2
    while m >= 1:
        levels.append(m)
        m //= 2
    t = np.arange(chunk)
    tri = (t[None, :] <= t[:, None]).astype(np.float32)
    mats = [tri]
    for m in levels:
        ref = (t // (2 * m)) * (2 * m) + m - 1
        sel = np.zeros((chunk, chunk), np.float32)
        sel[t, ref] = 1.0
        mats.append((np.eye(chunk, dtype=np.float32) - sel) @ tri)
    return levels, np.concatenate(mats, axis=0)


def _hgrn_kernel(p_ref, f_ref, lbl_ref, ng_ref, seg_ref, o_ref, state_ref, *, chunk, d_model,
                 layer, levels):
    C = chunk
    heads = d_model // H_KDIM

    @pl.when(pl.program_id(1) == 0)
    def _():
        state_ref[...] = jnp.zeros_like(state_ref)

    logits = lbl_ref[...]
    mx = jnp.max(logits, axis=0, keepdims=True)
    ex = jnp.exp(logits - mx)
    den = jnp.sum(ex, axis=0, keepdims=True)
    lb = jnp.zeros((1, d_model), F32)
    for l in range(1, layer + 1):
        lb = lb + ex[l:l + 1, :] / den

    q = _silu(p_ref[:, 0:d_model].astype(F32))
    v = p_ref[:, d_model:2 * d_model].astype(F32)
    gate = p_ref[:, 2 * d_model:3 * d_model].astype(F32)
    f = f_ref[...]

    a = jnp.log(lb)
    log_sig = jnp.minimum(f, 0.0) - jnp.log(1.0 + jnp.exp(-jnp.abs(f)))
    b = jnp.log(1.0 - lb) + log_sig
    log_f = jnp.maximum(a, b) + jnp.log(1.0 + jnp.exp(-jnp.abs(a - b)))
    k = (1.0 - lb) * _sigmoid(-f)

    segs = _dot_sel(seg_ref[...], log_f)
    G = segs[0:C]
    row = lax.broadcasted_iota(jnp.int32, (C, 1), 0)
    trow = lax.broadcasted_iota(jnp.int32, (C, C), 0)
    tcol = lax.broadcasted_iota(jnp.int32, (C, C), 1)

    scores = [jnp.where(trow == tcol, _dot_nt(
        (q[:, h * H_KDIM:(h + 1) * H_KDIM]).astype(BF16),
        (k[:, h * H_KDIM:(h + 1) * H_KDIM]).astype(BF16)), 0.0) for h in range(heads)]
    for li, m in enumerate(levels):
        d = segs[(li + 1) * C:(li + 2) * C]
        e = jnp.exp(-jnp.abs(d))
        shift = m.bit_length() - 1
        upper = ((row >> shift) & 1) == 1
        qe = jnp.where(upper, q * e, 0.0).astype(BF16)
        ke = jnp.where(upper, 0.0, k * e).astype(BF16)
        same = (trow >> (shift + 1)) == (tcol >> (shift + 1))
        for h in range(heads):
            sl = slice(h * H_KDIM, (h + 1) * H_KDIM)
            scores[h] = scores[h] + jnp.where(same, _dot_nt(qe[:, sl], ke[:, sl]), 0.0)

    g_last = G[C - 1:C, :]
    q_in = (q * jnp.exp(G)).astype(BF16)
    k_end = (k * jnp.exp(g_last - G)).astype(BF16)
    v_bf = v.astype(BF16)
    decay_last = jnp.exp(g_last)
    outs = []
    for h in range(heads):
        sl = slice(h * H_KDIM, (h + 1) * H_KDIM)
        st = state_ref[h]
        o_h = _dot(scores[h].astype(BF16), v_bf[:, sl]) + _dot_nt(q_in[:, sl], st.astype(BF16))
        state_ref[h] = st * decay_last[:, sl] + _dot_tn(v_bf[:, sl], k_end[:, sl])
        outs.append(_rms(o_h))
    o = jnp.concatenate(outs, axis=1) * ng_ref[...]
    o_ref[...] = (o * _silu(gate)).astype(o_ref.dtype)


def _hgrn_core(proj, f_pre, lb_logits, norm_g, layer, d_model, chunk=64):
    bsz, seq, n = proj.shape
    depth = lb_logits.shape[0]
    heads = d_model // H_KDIM
    levels, seg = _hgrn_level_tables(chunk)
    seg = jnp.asarray(seg, dtype=BF16)
    const = lambda b, i: (0, 0)
    return pl.pallas_call(
        functools.partial(_hgrn_kernel, chunk=chunk, d_model=d_model, layer=layer,
                          levels=tuple(levels)),
        grid=(bsz, seq // chunk),
        in_specs=[
            pl.BlockSpec((None, chunk, n), lambda b, i: (b, i, 0)),
            pl.BlockSpec((None, chunk, d_model), lambda b, i: (b, i, 0)),
            pl.BlockSpec((depth, d_model), const),
            pl.BlockSpec((1, d_model), const),
            pl.BlockSpec(seg.shape, const),
        ],
        out_specs=pl.BlockSpec((None, chunk, d_model), lambda b, i: (b, i, 0)),
        out_shape=jax.ShapeDtypeStruct((bsz, seq, d_model), BF16),
        scratch_shapes=[pltpu.VMEM((heads, H_KDIM, H_KDIM), F32)],
        compiler_params=_cparams(("parallel", "arbitrary")),
        name="hgrn_core",
    )(proj, f_pre, lb_logits, norm_g, seg)


def _pad_cols(a, n):
    return jnp.pad(a, ((0, 0), (0, n - a.shape[1])))


def kernel(x, c, ada_w, ada_b, norm_g, ffn_w_in, ffn_w_out, m_w_in, m_conv_w, m_conv_b, m_dt_bias,
           m_a_log, m_d, m_norm_g, m_w_out, h_w_in, h_lb_logits, h_norm_g, h_w_out, s_w_in,
           s_w_out):
    bsz, seq, d = x.shape
    depth = ada_w.shape[0]
    hidden = ffn_w_out.shape[1]
    d_inner = m_w_out.shape[1]
    m_heads = m_dt_bias.shape[1]
    conv_dim = m_conv_w.shape[2]

    rows = -(-bsz // SUBLANES) * SUBLANES
    c_pad = jnp.pad(c, ((0, rows - bsz), (0, 0)))
    mod = _adaln(c_pad, ada_w, ada_b)[:, :bsz]
    mod = mod.reshape(depth, bsz, N_MOD, 1, d)

    for layer in range(depth):
        sh_m, sc_m, gt_m, sh_f, sc_f, gt_f = (mod[layer, :, i] for i in range(N_MOD))
        g = norm_g[layer].reshape(4, 1, d)
        kind, j = layer % N_MIXERS, layer // N_MIXERS
        if kind == 0:
            w = m_w_in[j]
            w_z = w[:, :d_inner]
            w_xbc = w[:, d_inner:d_inner + conv_dim]
            w_dt = _pad_cols(w[:, d_inner + conv_dim:], LANES)
            w_cat = jnp.concatenate([w_xbc, w_z], axis=1).astype(BF16)
            proj, dt = _inproj(x, g[0], sh_m, sc_m, w_cat, w_dt.astype(BF16))
            a = _mamba_core(
                proj, dt, m_conv_w[j], m_conv_b[j].reshape(1, conv_dim),
                _pad_cols(m_dt_bias[j].reshape(1, m_heads), LANES),
                _pad_cols(m_a_log[j].reshape(1, m_heads), LANES),
                jnp.repeat(m_d[j], M_HEAD_DIM).reshape(1, d_inner),
                m_norm_g[j].reshape(1, d_inner), d_inner)
            w_out = m_w_out[j]
        elif kind == 1:
            w = h_w_in[j].astype(BF16)
            w_qig = jnp.concatenate([w[:, :d], w[:, 2 * d:]], axis=1)
            proj, f_pre = _inproj(x, g[0], sh_m, sc_m, w_qig, w[:, d:2 * d])
            a = _hgrn_core(proj, f_pre, h_lb_logits, h_norm_g[j].reshape(1, d), layer, d)
            w_out = h_w_out[j]
        else:
            qkv = _inproj(x, g[0], sh_m, sc_m, s_w_in[j].astype(BF16))
            a = _stickbreak(qkv, d)
            w_out = s_w_out[j]
        w_in = ffn_w_in[layer]
        x = _outffn(a, w_out.astype(BF16), x, g[1], gt_m, g[2], sh_f, sc_f,
                    w_in[:, :hidden].astype(BF16), w_in[:, hidden:].astype(BF16),
                    ffn_w_out[layer].astype(BF16), g[3], gt_f)
    return x
```
